```python
import jax, jax.numpy as jnp
from jax import lax
import numpy as np

D_MODEL = 2048
BATCH = 16
SEQ = 2048
DEPTH = 1

CHUNK = 64
D_MIX = D_MODEL
D_CONV = D_MIX // 2
CONV_K = 31
D_SSD = D_MIX - D_CONV
SSD_HEAD_DIM = 64
SSD_HEADS = D_SSD // SSD_HEAD_DIM
SSD_GROUPS = 4
SSD_STATE = 128
SSD_CONV_K = 4
D_XBC = D_SSD + 2 * SSD_GROUPS * SSD_STATE
D_IN = 2 * D_CONV + D_SSD + D_XBC + SSD_HEADS
D_FF = -(-(8 * D_MODEL) // (3 * 256)) * 256
EPS = 1e-6

kernel_name = "hybrid_conformer_conv_mamba2_ssd_block"


def rmsnorm(x, g):
    xf = x.astype(jnp.float32)
    y = xf * lax.rsqrt(jnp.mean(xf * xf, axis=-1, keepdims=True) + EPS)
    return (y * g.astype(jnp.float32)).astype(x.dtype)


def layernorm(x, g, b):
    xf = x.astype(jnp.float32)
    mu = jnp.mean(xf, axis=-1, keepdims=True)
    var = jnp.mean(jnp.square(xf - mu), axis=-1, keepdims=True)
    y = (xf - mu) * lax.rsqrt(var + EPS)
    return (y * g.astype(jnp.float32) + b.astype(jnp.float32)).astype(x.dtype)


def causal_depthwise_conv(u, w, b):
    k, c = w.shape
    out = lax.conv_general_dilated(
        u, w[:, None, :].astype(u.dtype), window_strides=(1,), padding=[(k - 1, 0)],
        dimension_numbers=("NWC", "WIO", "NWC"), feature_group_count=c)
    return out + b.astype(u.dtype)


def segsum(a):
    cs = jnp.cumsum(a, axis=-1)
    diff = cs[..., :, None] - cs[..., None, :]
    l = a.shape[-1]
    mask = jnp.tril(jnp.ones((l, l), dtype=bool))
    return jnp.where(mask, diff, -jnp.inf)


def ssd_chunked(xh, dt, a_head, bm, cm):
    bsz, s, h, p = xh.shape
    g, n = bm.shape[2], bm.shape[3]
    r = h // g
    nc = s // CHUNK
    xd = (xh * dt[..., None]).reshape(bsz, nc, CHUNK, g, r, p)
    a = (dt * a_head).reshape(bsz, nc, CHUNK, g, r).transpose(0, 3, 4, 1, 2)
    bc = bm.reshape(bsz, nc, CHUNK, g, n)
    cc = cm.reshape(bsz, nc, CHUNK, g, n)
    a_cs = jnp.cumsum(a, axis=-1)
    decay_mat = jnp.exp(segsum(a))
    scores = jnp.einsum("bclgn,bcsgn->bgcls", cc, bc)
    y_diag = jnp.einsum("bgrcls,bcsgrp->bclgrp", scores[:, :, None] * decay_mat, xd)
    decay_states = jnp.exp(a_cs[..., -1:] - a_cs)
    states = jnp.einsum("bclgn,bgrcl,bclgrp->bcgrpn", bc, decay_states, xd)
    chunk_decay = jnp.exp(a_cs[..., -1])

    def step(hstate, inp):
        st, dec = inp
        return hstate * dec[..., None, None] + st, hstate

    h0 = jnp.zeros((bsz, g, r, p, n), dtype=xd.dtype)
    _, prev = lax.scan(step, h0, (states.transpose(1, 0, 2, 3, 4, 5),
                                  chunk_decay.transpose(3, 0, 1, 2)))
    prev = prev.transpose(1, 0, 2, 3, 4, 5)
    y_off = jnp.einsum("bclgn,bcgrpn,bgrcl->bclgrp", cc, prev, jnp.exp(a_cs))
    return (y_diag + y_off).reshape(bsz, s, h, p)


def setup_inputs(seed: int = 0) -> dict:
    key = jax.random.key(seed)
    ks = jax.random.split(key, 24)
    f32 = jnp.float32

    def nrm(k, shape, scale):
        return jax.random.normal(k, shape, f32) * scale

    def gain(k, shape):
        return 1.0 + 0.05 * jax.random.normal(k, shape, f32)

    L = DEPTH
    x = jax.random.normal(ks[0], (BATCH, SEQ, D_MODEL), f32)
    norm_mix_pre = gain(ks[1], (L, D_MODEL))
    w_in = nrm(ks[2], (L, D_MODEL, D_IN), D_MODEL ** -0.5)
    conv_dw_w = nrm(ks[3], (L, CONV_K, D_CONV), CONV_K ** -0.5)
    conv_dw_b = nrm(ks[4], (L, D_CONV), 0.02)
    conv_ln_g = gain(ks[5], (L, D_CONV))
    conv_ln_b = nrm(ks[6], (L, D_CONV), 0.02)
    ssd_conv_w = nrm(ks[7], (L, SSD_CONV_K, D_XBC), SSD_CONV_K ** -0.5)
    ssd_conv_b = nrm(ks[8], (L, D_XBC), 0.02)
    dt0 = jnp.exp(jax.random.uniform(ks[9], (L, SSD_HEADS), f32,
                                     minval=np.log(1e-3), maxval=np.log(1e-1)))
    ssd_dt_bias = dt0 + jnp.log(-jnp.expm1(-dt0))
    ssd_a_log = jnp.log(jax.random.uniform(ks[10], (L, SSD_HEADS), f32, minval=1.0, maxval=16.0))
    ssd_d = gain(ks[11], (L, SSD_HEADS))
    ssd_norm_w = gain(ks[12], (L, D_SSD))
    w_out = nrm(ks[13], (L, D_MIX, D_MODEL), D_MIX ** -0.5)
    norm_mix_post = gain(ks[14], (L, D_MODEL))
    norm_ffn_pre = gain(ks[15], (L, D_MODEL))
    w_gate = nrm(ks[16], (L, D_MODEL, D_FF), D_MODEL ** -0.5)
    w_up = nrm(ks[17], (L, D_MODEL, D_FF), D_MODEL ** -0.5)
    w_down = nrm(ks[18], (L, D_FF, D_MODEL), D_FF ** -0.5)
    norm_ffn_post = gain(ks[19], (L, D_MODEL))
    return {"x": x, "norm_mix_pre": norm_mix_pre, "w_in": w_in,
            "conv_dw_w": conv_dw_w, "conv_dw_b": conv_dw_b,
            "conv_ln_g": conv_ln_g, "conv_ln_b": conv_ln_b,
            "ssd_conv_w": ssd_conv_w, "ssd_conv_b": ssd_conv_b,
            "ssd_dt_bias": ssd_dt_bias, "ssd_a_log": ssd_a_log, "ssd_d": ssd_d,
            "ssd_norm_w": ssd_norm_w, "w_out": w_out, "norm_mix_post": norm_mix_post,
            "norm_ffn_pre": norm_ffn_pre, "w_gate": w_gate, "w_up": w_up,
            "w_down": w_down, "norm_ffn_post": norm_ffn_post}


def reference(x, norm_mix_pre, w_in, conv_dw_w, conv_dw_b, conv_ln_g, conv_ln_b,
              ssd_conv_w, ssd_conv_b, ssd_dt_bias, ssd_a_log, ssd_d, ssd_norm_w,
              w_out, norm_mix_post, norm_ffn_pre, w_gate, w_up, w_down, norm_ffn_post):
    bsz, s, _ = x.shape
    f32 = jnp.float32
    for l in range(DEPTH):
        h = rmsnorm(x, norm_mix_pre[l])
        proj = h @ w_in[l]
        c_a, c_g, z, xbc, dt_raw = jnp.split(
            proj, [D_CONV, 2 * D_CONV, 2 * D_CONV + D_SSD, 2 * D_CONV + D_SSD + D_XBC], axis=-1)

        u = c_a * jax.nn.sigmoid(c_g)
        u = causal_depthwise_conv(u, conv_dw_w[l], conv_dw_b[l])
        u = jax.nn.silu(layernorm(u, conv_ln_g[l], conv_ln_b[l]))

        xbc = jax.nn.silu(causal_depthwise_conv(xbc, ssd_conv_w[l], ssd_conv_b[l]))
        xs, bm, cm = jnp.split(xbc, [D_SSD, D_SSD + SSD_GROUPS * SSD_STATE], axis=-1)
        xh = xs.astype(f32).reshape(bsz, s, SSD_HEADS, SSD_HEAD_DIM)
        bm = bm.astype(f32).reshape(bsz, s, SSD_GROUPS, SSD_STATE)
        cm = cm.astype(f32).reshape(bsz, s, SSD_GROUPS, SSD_STATE)
        dt = jax.nn.softplus(dt_raw.astype(f32) + ssd_dt_bias[l].astype(f32))
        a_head = -jnp.exp(ssd_a_log[l].astype(f32))
        y = ssd_chunked(xh, dt, a_head, bm, cm)
        y = y + xh * ssd_d[l].astype(f32)[:, None]
        y = y.reshape(bsz, s, D_SSD) * jax.nn.silu(z.astype(f32))
        yg = y.reshape(bsz, s, SSD_GROUPS, D_SSD // SSD_GROUPS)
        yg = yg * lax.rsqrt(jnp.mean(yg * yg, axis=-1, keepdims=True) + EPS)
        y = (yg.reshape(bsz, s, D_SSD) * ssd_norm_w[l].astype(f32)).astype(x.dtype)

        mix = jnp.concatenate([u, y], axis=-1) @ w_out[l]
        x = x + rmsnorm(mix, norm_mix_post[l])

        h = rmsnorm(x, norm_ffn_pre[l])
        f = (jax.nn.silu(h @ w_gate[l]) * (h @ w_up[l])) @ w_down[l]
        x = x + rmsnorm(f, norm_ffn_post[l])
    return x
```

```python
import functools

import jax
import jax.numpy as jnp
from jax import lax
from jax.experimental import pallas as pl
from jax.experimental.pallas import tpu as pltpu

D_MODEL = 2048
D_CONV = 1024
CONV_K = 31
D_SSD = 1024
SSD_HEAD_DIM = 64
SSD_HEADS = 16
SSD_GROUPS = 4
SSD_STATE = 128
SSD_CONV_K = 4
D_XBC = D_SSD + 2 * SSD_GROUPS * SSD_STATE
D_MAIN = 2 * D_CONV + D_SSD + D_XBC
D_FF = 5632
EPS = 1e-6

LANES = 128
SUBLANES = 8
VMEM_LIMIT = 56 * 1024 * 1024

SSD_L = 128
CONV_HALO = 32
XBC_HALO = 8

F32 = jnp.float32
BF16 = jnp.bfloat16


def _sigmoid(v):
    return 1.0 / (1.0 + jnp.exp(-v))


def _silu(v):
    return v * _sigmoid(v)


def _rms_scale(v):
    return lax.rsqrt(jnp.mean(v * v, axis=-1, keepdims=True) + EPS)


def _inproj_kernel(x_ref, g_ref, w_ref, wdt_ref, p_ref, dt_ref, h_scr):
    @pl.when(pl.program_id(1) == 0)
    def _():
        x = x_ref[...]
        h = (x * _rms_scale(x) * g_ref[...]).astype(BF16)
        h_scr[...] = h
        dt_ref[...] = jnp.dot(h, wdt_ref[...], preferred_element_type=F32)

    p_ref[...] = jnp.dot(h_scr[...], w_ref[...],
                         preferred_element_type=F32).astype(p_ref.dtype)


def _inproj(x2, g, w_main, w_dt, tm=1024, tn=1024):
    t = x2.shape[0]
    return pl.pallas_call(
        _inproj_kernel,
        name="inproj",
        out_shape=(jax.ShapeDtypeStruct((t, D_MAIN), BF16),
                   jax.ShapeDtypeStruct((t, LANES), F32)),
        grid=(t // tm, D_MAIN // tn),
        in_specs=[
            pl.BlockSpec((tm, D_MODEL), lambda i, j: (i, 0)),
            pl.BlockSpec((1, D_MODEL), lambda i, j: (0, 0)),
            pl.BlockSpec((D_MODEL, tn), lambda i, j: (0, j)),
            pl.BlockSpec((D_MODEL, LANES), lambda i, j: (0, 0)),
        ],
        out_specs=(pl.BlockSpec((tm, tn), lambda i, j: (i, j)),
                   pl.BlockSpec((tm, LANES), lambda i, j: (i, 0))),
        scratch_shapes=[pltpu.VMEM((tm, D_MODEL), BF16)],
        compiler_params=pltpu.CompilerParams(
            dimension_semantics=("arbitrary", "arbitrary"),
            vmem_limit_bytes=VMEM_LIMIT),
    )(x2, g, w_main, w_dt)


def _mixer_kernel(a_ref, g_ref, z_ref, xs_ref, bc_ref, dt_ref,
                  cw_ref, cb_ref, lng_ref, lnb_ref, sw_ref, sb_ref,
                  dtb_ref, alog_ref, dexp_ref, nw_ref,
                  mix_ref,
                  ubuf, cbuf, xbuf, xcbuf, ybuf, st_ref, *, ts):
    @pl.when(pl.program_id(1) == 0)
    def _():
        ubuf[0:CONV_HALO, :] = jnp.zeros((CONV_HALO, D_CONV), F32)
        xbuf[0:XBC_HALO, :] = jnp.zeros((XBC_HALO, D_XBC), F32)
        st_ref[...] = jnp.zeros(st_ref.shape, F32)

    ubuf[CONV_HALO:CONV_HALO + ts, :] = (
        a_ref[...].astype(F32) * _sigmoid(g_ref[...].astype(F32)))

    rb = 64
    off0 = CONV_HALO - (CONV_K - 1)

    def conv_cols(cb, carry):
        c0 = pl.multiple_of(cb * LANES, LANES)
        for r in range(ts // rb):
            acc = jnp.broadcast_to(cb_ref[:, pl.ds(c0, LANES)], (rb, LANES))
            for k in range(CONV_K):
                acc = acc + (cw_ref[pl.ds(k, 1), pl.ds(c0, LANES)]
                             * ubuf[pl.ds(r * rb + off0 + k, rb), pl.ds(c0, LANES)])
            cbuf[pl.ds(r * rb, rb), pl.ds(c0, LANES)] = acc
        return carry

    lax.fori_loop(0, D_CONV // LANES, conv_cols, 0)
    ubuf[0:CONV_HALO, :] = ubuf[ts:ts + CONV_HALO, :]

    c = cbuf[...]
    mu = jnp.mean(c, axis=-1, keepdims=True)
    d = c - mu
    var = jnp.mean(d * d, axis=-1, keepdims=True)
    u = d * lax.rsqrt(var + EPS) * lng_ref[...] + lnb_ref[...]
    mix_ref[:, 0:D_CONV] = _silu(u).astype(mix_ref.dtype)

    xbuf[XBC_HALO:XBC_HALO + ts, 0:D_SSD] = xs_ref[...].astype(F32)
    xbuf[XBC_HALO:XBC_HALO + ts, D_SSD:D_XBC] = bc_ref[...].astype(F32)
    xoff = XBC_HALO - (SSD_CONV_K - 1)
    xc = jnp.broadcast_to(sb_ref[...], (ts, D_XBC))
    for k in range(SSD_CONV_K):
        xc = xc + sw_ref[pl.ds(k, 1), :] * xbuf[pl.ds(xoff + k, ts), :]
    xcbuf[...] = _silu(xc)
    xbuf[0:XBC_HALO, :] = xbuf[ts:ts + XBC_HALO, :]

    dtr = dt_ref[...] + dtb_ref[...]
    dt = jnp.maximum(dtr, 0.0) + jnp.log1p(jnp.exp(-jnp.abs(dtr)))
    a = dt * (-jnp.exp(alog_ref[...]))

    L = SSD_L
    rows = lax.broadcasted_iota(jnp.int32, (L, L), 0)
    cols = lax.broadcasted_iota(jnp.int32, (L, L), 1)
    causal = rows >= cols
    tri = jnp.where(causal, 1.0, 0.0).astype(F32)
    lane = lax.broadcasted_iota(jnp.int32, (1, LANES), 1)
    lo_half = lane < SSD_HEAD_DIM
    n_off = D_SSD
    c_off = D_SSD + SSD_GROUPS * SSD_STATE

    for ci in range(ts // L):
        r0 = ci * L
        a_c = a[r0:r0 + L, :]
        dt_c = dt[r0:r0 + L, :]
        acs = jnp.dot(tri, a_c, preferred_element_type=F32,
                      precision=lax.Precision.HIGHEST)
        last = acs[L - 1:L, :]
        w_c = jnp.exp(last - acs) * dt_c
        cd = jnp.exp(last)
        acs_t = acs.T
        dt_t = dt_c.T
        for g in range(SSD_GROUPS):
            bg = xcbuf[r0:r0 + L, n_off + g * SSD_STATE:n_off + (g + 1) * SSD_STATE]
            cg = xcbuf[r0:r0 + L, c_off + g * SSD_STATE:c_off + (g + 1) * SSD_STATE]
            scores = lax.dot_general(cg.astype(BF16), bg.astype(BF16),
                                     (((1,), (1,)), ((), ())),
                                     preferred_element_type=F32)
            xw_parts = []
            cd_parts = []
            for j in range(2):
                h0 = g * 4 + 2 * j
                x_pair = xcbuf[r0:r0 + L, h0 * SSD_HEAD_DIM:h0 * SSD_HEAD_DIM + LANES]
                p_pair = st_ref[g, :, j * LANES:(j + 1) * LANES]
                rhs = jnp.concatenate([x_pair, p_pair], axis=0).astype(BF16)
                y_pair = jnp.zeros((L, LANES), F32)
                for hh in range(2):
                    h = h0 + hh
                    colb = jnp.broadcast_to(acs[:, h:h + 1], (L, L))
                    rowb = acs_t[h:h + 1, :]
                    decay = jnp.exp(jnp.where(causal, colb - rowb, -jnp.inf))
                    a1 = (scores * decay * dt_t[h:h + 1, :]).astype(BF16)
                    a2 = (cg * jnp.exp(colb)).astype(BF16)
                    lhs = jnp.concatenate([a1, a2], axis=1)
                    keep = lo_half if hh == 0 else jnp.logical_not(lo_half)
                    rhs_h = jnp.where(keep, rhs, jnp.zeros_like(rhs))
                    y_pair = y_pair + jnp.dot(lhs, rhs_h, preferred_element_type=F32)
                ybuf[r0:r0 + L, h0 * SSD_HEAD_DIM:h0 * SSD_HEAD_DIM + LANES] = y_pair
                w0 = jnp.broadcast_to(w_c[:, h0:h0 + 1], (L, LANES))
                w1 = jnp.broadcast_to(w_c[:, h0 + 1:h0 + 2], (L, LANES))
                xw_parts.append((x_pair * jnp.where(lo_half, w0, w1)).astype(BF16))
                c0v = jnp.broadcast_to(cd[:, h0:h0 + 1], (1, LANES))
                c1v = jnp.broadcast_to(cd[:, h0 + 1:h0 + 2], (1, LANES))
                cd_parts.append(jnp.where(lo_half, c0v, c1v))
            xw = jnp.concatenate(xw_parts, axis=1)
            cdx = jnp.concatenate(cd_parts, axis=1)
            bg_t = bg.T.astype(BF16)
            st_ref[g] = st_ref[g] * cdx + jnp.dot(bg_t, xw, preferred_element_type=F32)

    xs = xcbuf[:, 0:D_SSD]
    y = (ybuf[...] + xs * dexp_ref[...]) * _silu(z_ref[...].astype(F32))
    gw = D_SSD // SSD_GROUPS
    for g in range(SSD_GROUPS):
        yg = y[:, g * gw:(g + 1) * gw]
        yn = yg * _rms_scale(yg) * nw_ref[:, g * gw:(g + 1) * gw]
        mix_ref[:, D_CONV + g * gw:D_CONV + (g + 1) * gw] = yn.astype(mix_ref.dtype)


def _mixer(p, dt_raw, cw, cb, lng, lnb, sw, sb, dtb, alog, dexp, nw, bsz, seq, ts=256):
    t = p.shape[0]
    nt = seq // ts
    row = lambda b, s: b * nt + s
    full = lambda shape: pl.BlockSpec(shape, lambda b, s: (0, 0))
    col_block = lambda j: pl.BlockSpec((ts, 1024), lambda b, s: (row(b, s), j))
    return pl.pallas_call(
        functools.partial(_mixer_kernel, ts=ts),
        name="mixer",
        out_shape=jax.ShapeDtypeStruct((t, D_MODEL), BF16),
        grid=(bsz, nt),
        in_specs=[
            col_block(0), col_block(1), col_block(2), col_block(3), col_block(4),
            pl.BlockSpec((ts, LANES), lambda b, s: (row(b, s), 0)),
            full((CONV_K, D_CONV)), full((1, D_CONV)), full((1, D_CONV)), full((1, D_CONV)),
            full((SSD_CONV_K, D_XBC)), full((1, D_XBC)),
            full((1, LANES)), full((1, LANES)), full((1, D_SSD)), full((1, D_SSD)),
        ],
        out_specs=pl.BlockSpec((ts, D_MODEL), lambda b, s: (row(b, s), 0)),
        scratch_shapes=[
            pltpu.VMEM((CONV_HALO + ts, D_CONV), F32),
            pltpu.VMEM((ts, D_CONV), F32),
            pltpu.VMEM((XBC_HALO + ts, D_XBC), F32),
            pltpu.VMEM((ts, D_XBC), F32),
            pltpu.VMEM((ts, D_SSD), F32),
            pltpu.VMEM((SSD_GROUPS, SSD_STATE, 4 * SSD_HEAD_DIM), F32),
        ],
        compiler_params=pltpu.CompilerParams(
            dimension_semantics=("arbitrary", "arbitrary"),
            vmem_limit_bytes=VMEM_LIMIT),
    )(p, p, p, p, p, dt_raw, cw, cb, lng, lnb, sw, sb, dtb, alog, dexp, nw)


def _outproj_kernel(mix_ref, w_ref, x_ref, gpost_ref, gpre_ref, x1_ref, h_ref):
    m = jnp.dot(mix_ref[...], w_ref[...], preferred_element_type=F32)
    x1 = x_ref[...] + m * _rms_scale(m) * gpost_ref[...]
    x1_ref[...] = x1
    h_ref[...] = (x1 * _rms_scale(x1) * gpre_ref[...]).astype(h_ref.dtype)


def _outproj(mix, w_out, x2, gpost, gpre, tm=512):
    t = mix.shape[0]
    return pl.pallas_call(
        _outproj_kernel,
        name="outproj",
        out_shape=(jax.ShapeDtypeStruct((t, D_MODEL), F32),
                   jax.ShapeDtypeStruct((t, D_MODEL), BF16)),
        grid=(t // tm,),
        in_specs=[
            pl.BlockSpec((tm, D_MODEL), lambda i: (i, 0)),
            pl.BlockSpec((D_MODEL, D_MODEL), lambda i: (0, 0)),
            pl.BlockSpec((tm, D_MODEL), lambda i: (i, 0)),
            pl.BlockSpec((1, D_MODEL), lambda i: (0, 0)),
            pl.BlockSpec((1, D_MODEL), lambda i: (0, 0)),
        ],
        out_specs=(pl.BlockSpec((tm, D_MODEL), lambda i: (i, 0)),
                   pl.BlockSpec((tm, D_MODEL), lambda i: (i, 0))),
        compiler_params=pltpu.CompilerParams(
            dimension_semantics=("arbitrary",),
            vmem_limit_bytes=VMEM_LIMIT),
    )(mix, w_out, x2, gpost, gpre)


def _ffn_kernel(h_ref, wg_ref, wu_ref, wd_ref, x1_ref, gpost_ref, o_ref, acc_ref):
    f = pl.program_id(1)
    h = h_ref[...]
    gate = jnp.dot(h, wg_ref[...], preferred_element_type=F32)
    up = jnp.dot(h, wu_ref[...], preferred_element_type=F32)
    act = (_silu(gate) * up).astype(BF16)
    part = jnp.dot(act, wd_ref[...], preferred_element_type=F32)

    @pl.when(f == 0)
    def _():
        acc_ref[...] = part

    @pl.when(f > 0)
    def _():
        acc_ref[...] += part

    @pl.when(f == pl.num_programs(1) - 1)
    def _():
        y = acc_ref[...]
        o_ref[...] = x1_ref[...] + y * _rms_scale(y) * gpost_ref[...]


def _ffn(h, wg, wu, wd, x1, gpost, tm=512, tf=512):
    t = h.shape[0]
    return pl.pallas_call(
        _ffn_kernel,
        name="ffn",
        out_shape=jax.ShapeDtypeStruct((t, D_MODEL), F32),
        grid=(t // tm, D_FF // tf),
        in_specs=[
            pl.BlockSpec((tm, D_MODEL), lambda i, f: (i, 0)),
            pl.BlockSpec((D_MODEL, tf), lambda i, f: (0, f)),
            pl.BlockSpec((D_MODEL, tf), lambda i, f: (0, f)),
            pl.BlockSpec((tf, D_MODEL), lambda i, f: (f, 0)),
            pl.BlockSpec((tm, D_MODEL), lambda i, f: (i, 0)),
            pl.BlockSpec((1, D_MODEL), lambda i, f: (0, 0)),
        ],
        out_specs=pl.BlockSpec((tm, D_MODEL), lambda i, f: (i, 0)),
        scratch_shapes=[pltpu.VMEM((tm, D_MODEL), F32)],
        compiler_params=pltpu.CompilerParams(
            dimension_semantics=("arbitrary", "arbitrary"),
            vmem_limit_bytes=VMEM_LIMIT),
    )(h, wg, wu, wd, x1, gpost)


def kernel(x, norm_mix_pre, w_in, conv_dw_w, conv_dw_b, conv_ln_g, conv_ln_b,
           ssd_conv_w, ssd_conv_b, ssd_dt_bias, ssd_a_log, ssd_d, ssd_norm_w,
           w_out, norm_mix_post, norm_ffn_pre, w_gate, w_up, w_down, norm_ffn_post):
    bsz, seq, _ = x.shape
    depth = w_in.shape[0]
    x2 = x.reshape(bsz * seq, D_MODEL)
    row = lambda v: v.reshape(1, -1)
    pad = lambda v: jnp.pad(v, (0, LANES - v.shape[0])).reshape(1, LANES)
    for l in range(depth):
        w_main = w_in[l][:, :D_MAIN].astype(BF16)
        w_dt = jnp.pad(w_in[l][:, D_MAIN:], ((0, 0), (0, LANES - SSD_HEADS))).astype(BF16)
        p, dt_raw = _inproj(x2, row(norm_mix_pre[l]), w_main, w_dt)
        mix = _mixer(
            p, dt_raw, conv_dw_w[l], row(conv_dw_b[l]), row(conv_ln_g[l]), row(conv_ln_b[l]),
            ssd_conv_w[l], row(ssd_conv_b[l]), pad(ssd_dt_bias[l]), pad(ssd_a_log[l]),
            row(jnp.repeat(ssd_d[l], SSD_HEAD_DIM)), row(ssd_norm_w[l]), bsz, seq)
        x1, h = _outproj(mix, w_out[l].astype(BF16), x2,
                         row(norm_mix_post[l]), row(norm_ffn_pre[l]))
        x2 = _ffn(h, w_gate[l].astype(BF16), w_up[l].astype(BF16), w_down[l].astype(BF16),
                  x1, row(norm_ffn_post[l]))
    return x2.reshape(bsz, seq, D_MODEL)
```

```python
import functools

import jax
import jax.numpy as jnp
from jax import lax
from jax.experimental import pallas as pl
from jax.experimental.pallas import tpu as pltpu

D_MODEL = 2048
D_CONV = 1024
CONV_K = 31
D_SSD = 1024
SSD_HEAD_DIM = 64
SSD_HEADS = 16
SSD_GROUPS = 4
SSD_STATE = 128
SSD_CONV_K = 4
D_XBC = D_SSD + 2 * SSD_GROUPS * SSD_STATE
D_MAIN = 2 * D_CONV + D_SSD + D_XBC
D_FF = 5632
EPS = 1e-6

LANES = 128
SUBLANES = 8
VMEM_LIMIT = 56 * 1024 * 1024

SSD_L = 128
CONV_HALO = 32
XBC_HALO = 8

F32 = jnp.float32
BF16 = jnp.bfloat16


def _sigmoid(v):
    return 0.5 * jnp.tanh(0.5 * v) + 0.5


def _silu(v):
    hv = 0.5 * v
    return hv * jnp.tanh(hv) + hv


def _rms_scale(v):
    return lax.rsqrt(jnp.mean(v * v, axis=-1, keepdims=True) + EPS)


def _inproj_kernel(x_ref, g_ref, w_ref, wdt_ref, p_ref, dt_ref, h_scr):
    @pl.when(pl.program_id(1) == 0)
    def _():
        x = x_ref[...]
        h = (x * _rms_scale(x) * g_ref[...]).astype(BF16)
        h_scr[...] = h
        dt_ref[...] = jnp.dot(h, wdt_ref[...], preferred_element_type=F32)

    p_ref[...] = jnp.dot(h_scr[...], w_ref[...],
                         preferred_element_type=F32).astype(p_ref.dtype)


def _inproj(x2, g, w_main, w_dt, tm=1024, tn=1024):
    t = x2.shape[0]
    return pl.pallas_call(
        _inproj_kernel,
        name="inproj",
        out_shape=(jax.ShapeDtypeStruct((t, D_MAIN), BF16),
                   jax.ShapeDtypeStruct((t, LANES), F32)),
        grid=(t // tm, D_MAIN // tn),
        in_specs=[
            pl.BlockSpec((tm, D_MODEL), lambda i, j: (i, 0)),
            pl.BlockSpec((1, D_MODEL), lambda i, j: (0, 0)),
            pl.BlockSpec((D_MODEL, tn), lambda i, j: (0, j)),
            pl.BlockSpec((D_MODEL, LANES), lambda i, j: (0, 0)),
        ],
        out_specs=(pl.BlockSpec((tm, tn), lambda i, j: (i, j)),
                   pl.BlockSpec((tm, LANES), lambda i, j: (i, 0))),
        scratch_shapes=[pltpu.VMEM((tm, D_MODEL), BF16)],
        compiler_params=pltpu.CompilerParams(
            dimension_semantics=("arbitrary", "arbitrary"),
            vmem_limit_bytes=VMEM_LIMIT),
    )(x2, g, w_main, w_dt)


def _mixer_kernel(a_ref, g_ref, z_ref, xs_ref, bc_ref, dt_ref,
                  cw_ref, cb_ref, lng_ref, lnb_ref, sw_ref, sb_ref,
                  dtb_ref, alog_ref, dexp_ref, nw_ref,
                  mix_ref,
                  ubuf, cbuf, xbuf, xcbuf, ybuf, st_ref, *, ts):
    @pl.when(pl.program_id(1) == 0)
    def _():
        ubuf[0:CONV_HALO, :] = jnp.zeros((CONV_HALO, D_CONV), F32)
        xbuf[0:XBC_HALO, :] = jnp.zeros((XBC_HALO, D_XBC), F32)
        st_ref[...] = jnp.zeros(st_ref.shape, F32)

    ubuf[CONV_HALO:CONV_HALO + ts, :] = (
        a_ref[...].astype(F32) * _sigmoid(g_ref[...].astype(F32)))

    rb = 64
    off0 = CONV_HALO - (CONV_K - 1)

    def conv_cols(cb, carry):
        c0 = pl.multiple_of(cb * LANES, LANES)
        for r in range(ts // rb):
            acc = jnp.broadcast_to(cb_ref[:, pl.ds(c0, LANES)], (rb, LANES))
            for k in range(CONV_K):
                acc = acc + (cw_ref[pl.ds(k, 1), pl.ds(c0, LANES)]
                             * ubuf[pl.ds(r * rb + off0 + k, rb), pl.ds(c0, LANES)])
            cbuf[pl.ds(r * rb, rb), pl.ds(c0, LANES)] = acc
        return carry

    lax.fori_loop(0, D_CONV // LANES, conv_cols, 0)
    ubuf[0:CONV_HALO, :] = ubuf[ts:ts + CONV_HALO, :]

    c = cbuf[...]
    mu = jnp.mean(c, axis=-1, keepdims=True)
    d = c - mu
    var = jnp.mean(d * d, axis=-1, keepdims=True)
    u = d * lax.rsqrt(var + EPS) * lng_ref[...] + lnb_ref[...]
    mix_ref[:, 0:D_CONV] = _silu(u).astype(mix_ref.dtype)

    xbuf[XBC_HALO:XBC_HALO + ts, 0:D_SSD] = xs_ref[...].astype(F32)
    xbuf[XBC_HALO:XBC_HALO + ts, D_SSD:D_XBC] = bc_ref[...].astype(F32)
    xoff = XBC_HALO - (SSD_CONV_K - 1)
    xc = jnp.broadcast_to(sb_ref[...], (ts, D_XBC))
    for k in range(SSD_CONV_K):
        xc = xc + sw_ref[pl.ds(k, 1), :] * xbuf[pl.ds(xoff + k, ts), :]
    xcbuf[...] = _silu(xc)
    xbuf[0:XBC_HALO, :] = xbuf[ts:ts + XBC_HALO, :]

    dtr = dt_ref[...] + dtb_ref[...]
    dt = jnp.maximum(dtr, 0.0) + jnp.log1p(jnp.exp(-jnp.abs(dtr)))
    a = dt * (-jnp.exp(alog_ref[...]))

    L = SSD_L
    rows = lax.broadcasted_iota(jnp.int32, (L, L), 0)
    cols = lax.broadcasted_iota(jnp.int32, (L, L), 1)
    causal = rows >= cols
    tri = jnp.where(causal, 1.0, 0.0).astype(F32)
    lane = lax.broadcasted_iota(jnp.int32, (1, LANES), 1)
    lo_half = lane < SSD_HEAD_DIM
    n_off = D_SSD
    c_off = D_SSD + SSD_GROUPS * SSD_STATE

    for ci in range(ts // L):
        r0 = ci * L
        a_c = a[r0:r0 + L, :]
        dt_c = dt[r0:r0 + L, :]
        acs = jnp.dot(tri, a_c, preferred_element_type=F32,
                      precision=lax.Precision.HIGHEST)
        last = acs[L - 1:L, :]
        w_c = jnp.exp(last - acs) * dt_c
        cd = jnp.exp(last)
        acs_t = acs.T
        dt_t = dt_c.T
        for g in range(SSD_GROUPS):
            bg = xcbuf[r0:r0 + L, n_off + g * SSD_STATE:n_off + (g + 1) * SSD_STATE]
            cg = xcbuf[r0:r0 + L, c_off + g * SSD_STATE:c_off + (g + 1) * SSD_STATE]
            scores = lax.dot_general(cg.astype(BF16), bg.astype(BF16),
                                     (((1,), (1,)), ((), ())),
                                     preferred_element_type=F32)
            xw_parts = []
            cd_parts = []
            for j in range(2):
                h0 = g * 4 + 2 * j
                x_pair = xcbuf[r0:r0 + L, h0 * SSD_HEAD_DIM:h0 * SSD_HEAD_DIM + LANES]
                p_pair = st_ref[g, :, j * LANES:(j + 1) * LANES]
                rhs = jnp.concatenate([x_pair, p_pair], axis=0).astype(BF16)
                y_pair = jnp.zeros((L, LANES), F32)
                for hh in range(2):
                    h = h0 + hh
                    colb = jnp.broadcast_to(acs[:, h:h + 1], (L, L))
                    rowb = acs_t[h:h + 1, :]
                    decay = jnp.exp(jnp.where(causal, colb - rowb, -jnp.inf))
                    a1 = (scores * decay * dt_t[h:h + 1, :]).astype(BF16)
                    a2 = (cg * jnp.exp(colb)).astype(BF16)
                    lhs = jnp.concatenate([a1, a2], axis=1)
                    keep = lo_half if hh == 0 else jnp.logical_not(lo_half)
                    rhs_h = jnp.where(keep, rhs, jnp.zeros_like(rhs))
                    y_pair = y_pair + jnp.dot(lhs, rhs_h, preferred_element_type=F32)
                ybuf[r0:r0 + L, h0 * SSD_HEAD_DIM:h0 * SSD_HEAD_DIM + LANES] = y_pair
                w0 = jnp.broadcast_to(w_c[:, h0:h0 + 1], (L, LANES))
                w1 = jnp.broadcast_to(w_c[:, h0 + 1:h0 + 2], (L, LANES))
                xw_parts.append((x_pair * jnp.where(lo_half, w0, w1)).astype(BF16))
                c0v = jnp.broadcast_to(cd[:, h0:h0 + 1], (1, LANES))
                c1v = jnp.broadcast_to(cd[:, h0 + 1:h0 + 2], (1, LANES))
                cd_parts.append(jnp.where(lo_half, c0v, c1v))
            xw = jnp.concatenate(xw_parts, axis=1)
            cdx = jnp.concatenate(cd_parts, axis=1)
            bg_t = bg.T.astype(BF16)
            st_ref[g] = st_ref[g] * cdx + jnp.dot(bg_t, xw, preferred_element_type=F32)

    xs = xcbuf[:, 0:D_SSD]
    y = (ybuf[...] + xs * dexp_ref[...]) * _silu(z_ref[...].astype(F32))
    gw = D_SSD // SSD_GROUPS
    for g in range(SSD_GROUPS):
        yg = y[:, g * gw:(g + 1) * gw]
        yn = yg * _rms_scale(yg) * nw_ref[:, g * gw:(g + 1) * gw]
        mix_ref[:, D_CONV + g * gw:D_CONV + (g + 1) * gw] = yn.astype(mix_ref.dtype)


def _mixer(p, dt_raw, cw, cb, lng, lnb, sw, sb, dtb, alog, dexp, nw, bsz, seq, ts=256):
    t = p.shape[0]
    nt = seq // ts
    row = lambda b, s: b * nt + s
    full = lambda shape: pl.BlockSpec(shape, lambda b, s: (0, 0))
    col_block = lambda j: pl.BlockSpec((ts, 1024), lambda b, s: (row(b, s), j))
    return pl.pallas_call(
        functools.partial(_mixer_kernel, ts=ts),
        name="mixer",
        out_shape=jax.ShapeDtypeStruct((t, D_MODEL), BF16),
        grid=(bsz, nt),
        in_specs=[
            col_block(0), col_block(1), col_block(2), col_block(3), col_block(4),
            pl.BlockSpec((ts, LANES), lambda b, s: (row(b, s), 0)),
            full((CONV_K, D_CONV)), full((1, D_CONV)), full((1, D_CONV)), full((1, D_CONV)),
            full((SSD_CONV_K, D_XBC)), full((1, D_XBC)),
            full((1, LANES)), full((1, LANES)), full((1, D_SSD)), full((1, D_SSD)),
        ],
        out_specs=pl.BlockSpec((ts, D_MODEL), lambda b, s: (row(b, s), 0)),
        scratch_shapes=[
            pltpu.VMEM((CONV_HALO + ts, D_CONV), F32),
            pltpu.VMEM((ts, D_CONV), F32),
            pltpu.VMEM((XBC_HALO + ts, D_XBC), F32),
            pltpu.VMEM((ts, D_XBC), F32),
            pltpu.VMEM((ts, D_SSD), F32),
            pltpu.VMEM((SSD_GROUPS, SSD_STATE, 4 * SSD_HEAD_DIM), F32),
        ],
        compiler_params=pltpu.CompilerParams(
            dimension_semantics=("arbitrary", "arbitrary"),
            vmem_limit_bytes=VMEM_LIMIT),
    )(p, p, p, p, p, dt_raw, cw, cb, lng, lnb, sw, sb, dtb, alog, dexp, nw)


def _outproj_kernel(mix_ref, w_ref, x_ref, gpost_ref, gpre_ref, x1_ref, h_ref):
    m = jnp.dot(mix_ref[...], w_ref[...], preferred_element_type=F32)
    x1 = x_ref[...] + m * _rms_scale(m) * gpost_ref[...]
    x1_ref[...] = x1
    h_ref[...] = (x1 * _rms_scale(x1) * gpre_ref[...]).astype(h_ref.dtype)


def _outproj(mix, w_out, x2, gpost, gpre, tm=512):
    t = mix.shape[0]
    return pl.pallas_call(
        _outproj_kernel,
        name="outproj",
        out_shape=(jax.ShapeDtypeStruct((t, D_MODEL), F32),
                   jax.ShapeDtypeStruct((t, D_MODEL), BF16)),
        grid=(t // tm,),
        in_specs=[
            pl.BlockSpec((tm, D_MODEL), lambda i: (i, 0)),
            pl.BlockSpec((D_MODEL, D_MODEL), lambda i: (0, 0)),
            pl.BlockSpec((tm, D_MODEL), lambda i: (i, 0)),
            pl.BlockSpec((1, D_MODEL), lambda i: (0, 0)),
            pl.BlockSpec((1, D_MODEL), lambda i: (0, 0)),
        ],
        out_specs=(pl.BlockSpec((tm, D_MODEL), lambda i: (i, 0)),
                   pl.BlockSpec((tm, D_MODEL), lambda i: (i, 0))),
        compiler_params=pltpu.CompilerParams(
            dimension_semantics=("arbitrary",),
            vmem_limit_bytes=VMEM_LIMIT),
    )(mix, w_out, x2, gpost, gpre)


def _ffn_kernel(h_ref, wg_ref, wu_ref, wd_ref, x1_hbm, gpost_ref, o_ref, x1_buf, x1_sem,
                *, tm, rows):
    i = pl.program_id(0)
    f = pl.program_id(1)
    x1_copy = pltpu.make_async_copy(
        x1_hbm.at[pl.ds(pl.multiple_of(i * tm, tm), tm), :], x1_buf, x1_sem)

    @pl.when(f == 0)
    def _():
        x1_copy.start()

    h = h_ref[...]
    gate = jnp.dot(h, wg_ref[...], preferred_element_type=F32)
    up = jnp.dot(h, wu_ref[...], preferred_element_type=F32)
    act = (_silu(gate) * up).astype(BF16)
    part = jnp.dot(act, wd_ref[...], preferred_element_type=F32)

    @pl.when(f == 0)
    def _():
        o_ref[...] = part

    @pl.when(f > 0)
    def _():
        o_ref[...] += part

    @pl.when(f == pl.num_programs(1) - 1)
    def _():
        x1_copy.wait()

        def norm_rows(r, carry):
            sl = pl.ds(pl.multiple_of(r * rows, rows), rows)
            y = o_ref[sl, :]
            o_ref[sl, :] = x1_buf[sl, :] + y * _rms_scale(y) * gpost_ref[...]
            return carry

        lax.fori_loop(0, tm // rows, norm_rows, 0)


def _ffn(h, wg, wu, wd, x1, gpost, tm=1024, tf=512, rows=128):
    t = h.shape[0]
    return pl.pallas_call(
        functools.partial(_ffn_kernel, tm=tm, rows=rows),
        name="ffn",
        out_shape=jax.ShapeDtypeStruct((t, D_MODEL), F32),
        grid=(t // tm, D_FF // tf),
        in_specs=[
            pl.BlockSpec((tm, D_MODEL), lambda i, f: (i, 0)),
            pl.BlockSpec((D_MODEL, tf), lambda i, f: (0, f)),
            pl.BlockSpec((D_MODEL, tf), lambda i, f: (0, f)),
            pl.BlockSpec((tf, D_MODEL), lambda i, f: (f, 0)),
            pl.BlockSpec(memory_space=pl.ANY),
            pl.BlockSpec((1, D_MODEL), lambda i, f: (0, 0)),
        ],
        out_specs=pl.BlockSpec((tm, D_MODEL), lambda i, f: (i, 0)),
        scratch_shapes=[pltpu.VMEM((tm, D_MODEL), F32), pltpu.SemaphoreType.DMA(())],
        compiler_params=pltpu.CompilerParams(
            dimension_semantics=("arbitrary", "arbitrary"),
            vmem_limit_bytes=VMEM_LIMIT),
    )(h, wg, wu, wd, x1, gpost)


def kernel(x, norm_mix_pre, w_in, conv_dw_w, conv_dw_b, conv_ln_g, conv_ln_b,
           ssd_conv_w, ssd_conv_b, ssd_dt_bias, ssd_a_log, ssd_d, ssd_norm_w,
           w_out, norm_mix_post, norm_ffn_pre, w_gate, w_up, w_down, norm_ffn_post):
    bsz, seq, _ = x.shape
    depth = w_in.shape[0]
    x2 = x.reshape(bsz * seq, D_MODEL)
    row = lambda v: v.reshape(1, -1)
    pad = lambda v: jnp.pad(v, (0, LANES - v.shape[0])).reshape(1, LANES)
    for l in range(depth):
        w_main = w_in[l][:, :D_MAIN].astype(BF16)
        w_dt = jnp.pad(w_in[l][:, D_MAIN:], ((0, 0), (0, LANES - SSD_HEADS))).astype(BF16)
        p, dt_raw = _inproj(x2, row(norm_mix_pre[l]), w_main, w_dt)
        mix = _mixer(
            p, dt_raw, conv_dw_w[l], row(conv_dw_b[l]), row(conv_ln_g[l]), row(conv_ln_b[l]),
            ssd_conv_w[l], row(ssd_conv_b[l]), pad(ssd_dt_bias[l]), pad(ssd_a_log[l]),
            row(jnp.repeat(ssd_d[l], SSD_HEAD_DIM)), row(ssd_norm_w[l]), bsz, seq)
        x1, h = _outproj(mix, w_out[l].astype(BF16), x2,
                         row(norm_mix_post[l]), row(norm_ffn_pre[l]))
        x2 = _ffn(h, w_gate[l].astype(BF16), w_up[l].astype(BF16), w_down[l].astype(BF16),
                  x1, row(norm_ffn_post[l]))
    return x2.reshape(bsz, seq, D_MODEL)
```

```python
import functools

import jax
import jax.numpy as jnp
from jax import lax
from jax.experimental import pallas as pl
from jax.experimental.pallas import tpu as pltpu

D_MODEL = 2048
D_CONV = 1024
CONV_K = 31
D_SSD = 1024
SSD_HEAD_DIM = 64
SSD_HEADS = 16
SSD_GROUPS = 4
SSD_STATE = 128
SSD_CONV_K = 4
D_XBC = D_SSD + 2 * SSD_GROUPS * SSD_STATE
D_MAIN = 2 * D_CONV + D_SSD + D_XBC
D_FF = 5632
EPS = 1e-6

LANES = 128
SUBLANES = 8
VMEM_LIMIT = 56 * 1024 * 1024

SSD_L = 128
CONV_HALO = 32
XBC_HALO = 8
ROW_STRIDE = 4

F32 = jnp.float32
BF16 = jnp.bfloat16


def _sigmoid(v):
    return 0.5 * jnp.tanh(0.5 * v) + 0.5


def _silu(v):
    hv = 0.5 * v
    return hv * jnp.tanh(hv) + hv


def _rms_scale(v):
    return lax.rsqrt(jnp.mean(v * v, axis=-1, keepdims=True) + EPS)


def _inproj_kernel(x_ref, g_ref, w_ref, wdt_ref, p_ref, dt_ref, h_scr):
    @pl.when(pl.program_id(1) == 0)
    def _():
        x = x_ref[...]
        h = (x * _rms_scale(x) * g_ref[...]).astype(BF16)
        h_scr[...] = h
        dt_ref[...] = jnp.dot(h, wdt_ref[...], preferred_element_type=F32)

    p_ref[...] = jnp.dot(h_scr[...], w_ref[...],
                         preferred_element_type=F32).astype(p_ref.dtype)


def _inproj(x2, g, w_main, w_dt, tm=1024, tn=1024):
    t = x2.shape[0]
    return pl.pallas_call(
        _inproj_kernel,
        name="inproj",
        out_shape=(jax.ShapeDtypeStruct((t, D_MAIN), BF16),
                   jax.ShapeDtypeStruct((t, LANES), F32)),
        grid=(t // tm, D_MAIN // tn),
        in_specs=[
            pl.BlockSpec((tm, D_MODEL), lambda i, j: (i, 0)),
            pl.BlockSpec((1, D_MODEL), lambda i, j: (0, 0)),
            pl.BlockSpec((D_MODEL, tn), lambda i, j: (0, j)),
            pl.BlockSpec((D_MODEL, LANES), lambda i, j: (0, 0)),
        ],
        out_specs=(pl.BlockSpec((tm, tn), lambda i, j: (i, j)),
                   pl.BlockSpec((tm, LANES), lambda i, j: (i, 0))),
        scratch_shapes=[pltpu.VMEM((tm, D_MODEL), BF16)],
        compiler_params=pltpu.CompilerParams(
            dimension_semantics=("arbitrary", "arbitrary"),
            vmem_limit_bytes=VMEM_LIMIT),
    )(x2, g, w_main, w_dt)


def _mixer_kernel(a_ref, g_ref, z_ref, xs_ref, bc_ref, dt_ref,
                  cw_ref, cb_ref, lng_ref, lnb_ref, sw_ref, sb_ref,
                  dtb_ref, alog_ref, dexp_ref, nw_ref,
                  mix_ref,
                  ubuf, cbuf, xbuf, xcbuf, ybuf, st_ref, *, ts):
    @pl.when(pl.program_id(1) == 0)
    def _():
        ubuf[:, 0:CONV_HALO, :] = jnp.zeros((D_CONV // LANES, CONV_HALO, LANES), F32)
        xbuf[:, 0:XBC_HALO, :] = jnp.zeros((D_XBC // LANES, XBC_HALO, LANES), F32)
        st_ref[...] = jnp.zeros(st_ref.shape, F32)

    glu = a_ref[...].astype(F32) * _sigmoid(g_ref[...].astype(F32))
    for cb in range(D_CONV // LANES):
        ubuf[cb, CONV_HALO:CONV_HALO + ts, :] = glu[:, cb * LANES:(cb + 1) * LANES]

    def causal_conv(buf, out_buf, w_ref, b_ref, n_taps, off, cb, post):
        cols = pl.ds(pl.multiple_of(cb * LANES, LANES), LANES)
        w = [w_ref[pl.ds(k, 1), cols] for k in range(n_taps)]
        bias = jnp.broadcast_to(b_ref[:, cols], (SUBLANES, LANES))
        grp = SUBLANES * ROW_STRIDE
        for base in range(0, ts, grp):
            accs = [bias] * ROW_STRIDE
            for s in range(n_taps + ROW_STRIDE - 1):
                tap = buf[cb, pl.ds(base + off + s, SUBLANES, stride=ROW_STRIDE), :]
                for j in range(ROW_STRIDE):
                    if 0 <= s - j < n_taps:
                        accs[j] = accs[j] + w[s - j] * tap
            for j in range(ROW_STRIDE):
                out_buf[cb, pl.ds(base + j, SUBLANES, stride=ROW_STRIDE), :] = post(accs[j])

    def conv_cols(cb, carry):
        causal_conv(ubuf, cbuf, cw_ref, cb_ref, CONV_K, CONV_HALO - (CONV_K - 1), cb,
                    lambda v: v)
        return carry

    lax.fori_loop(0, D_CONV // LANES, conv_cols, 0)
    ubuf[:, 0:CONV_HALO, :] = ubuf[:, ts:ts + CONV_HALO, :]

    c = jnp.concatenate([cbuf[cb] for cb in range(D_CONV // LANES)], axis=1)
    mu = jnp.mean(c, axis=-1, keepdims=True)
    d = c - mu
    var = jnp.mean(d * d, axis=-1, keepdims=True)
    u = d * lax.rsqrt(var + EPS) * lng_ref[...] + lnb_ref[...]
    mix_ref[:, 0:D_CONV] = _silu(u).astype(mix_ref.dtype)

    xs_in = xs_ref[...].astype(F32)
    bc_in = bc_ref[...].astype(F32)
    for cb in range(D_SSD // LANES):
        xbuf[cb, XBC_HALO:XBC_HALO + ts, :] = xs_in[:, cb * LANES:(cb + 1) * LANES]
        xbuf[D_SSD // LANES + cb, XBC_HALO:XBC_HALO + ts, :] = bc_in[:, cb * LANES:(cb + 1) * LANES]

    def xbc_cols(cb, carry):
        causal_conv(xbuf, xcbuf, sw_ref, sb_ref, SSD_CONV_K, XBC_HALO - (SSD_CONV_K - 1), cb,
                    _silu)
        return carry

    lax.fori_loop(0, D_XBC // LANES, xbc_cols, 0)
    xbuf[:, 0:XBC_HALO, :] = xbuf[:, ts:ts + XBC_HALO, :]

    dtr = dt_ref[...] + dtb_ref[...]
    dt = jnp.maximum(dtr, 0.0) + jnp.log1p(jnp.exp(-jnp.abs(dtr)))
    a = dt * (-jnp.exp(alog_ref[...]))

    L = SSD_L
    rows = lax.broadcasted_iota(jnp.int32, (L, L), 0)
    cols = lax.broadcasted_iota(jnp.int32, (L, L), 1)
    causal = rows >= cols
    tri = jnp.where(causal, 1.0, 0.0).astype(F32)
    lane = lax.broadcasted_iota(jnp.int32, (1, LANES), 1)
    lo_half = lane < SSD_HEAD_DIM
    b_blk = D_SSD // LANES
    c_blk = b_blk + SSD_GROUPS * SSD_STATE // LANES

    for ci in range(ts // L):
        r0 = ci * L
        a_c = a[r0:r0 + L, :]
        dt_c = dt[r0:r0 + L, :]
        acs = jnp.dot(tri, a_c, preferred_element_type=F32,
                      precision=lax.Precision.HIGHEST)
        last = acs[L - 1:L, :]
        w_c = jnp.exp(last - acs) * dt_c
        cd = jnp.exp(last)
        acs_t = acs.T
        dt_t = dt_c.T
        for g in range(SSD_GROUPS):
            bg = xcbuf[b_blk + g, r0:r0 + L, :]
            cg = xcbuf[c_blk + g, r0:r0 + L, :]
            scores = lax.dot_general(cg.astype(BF16), bg.astype(BF16),
                                     (((1,), (1,)), ((), ())),
                                     preferred_element_type=F32)
            xw_parts = []
            cd_parts = []
            for j in range(2):
                h0 = g * 4 + 2 * j
                x_pair = xcbuf[h0 * SSD_HEAD_DIM // LANES, r0:r0 + L, :]
                p_pair = st_ref[g, :, j * LANES:(j + 1) * LANES]
                rhs = jnp.concatenate([x_pair, p_pair], axis=0).astype(BF16)
                y_pair = jnp.zeros((L, LANES), F32)
                for hh in range(2):
                    h = h0 + hh
                    colb = jnp.broadcast_to(acs[:, h:h + 1], (L, L))
                    rowb = acs_t[h:h + 1, :]
                    decay = jnp.exp(jnp.where(causal, colb - rowb, -jnp.inf))
                    a1 = (scores * decay * dt_t[h:h + 1, :]).astype(BF16)
                    a2 = (cg * jnp.exp(colb)).astype(BF16)
                    lhs = jnp.concatenate([a1, a2], axis=1)
                    keep = lo_half if hh == 0 else jnp.logical_not(lo_half)
                    rhs_h = jnp.where(keep, rhs, jnp.zeros_like(rhs))
                    y_pair = y_pair + jnp.dot(lhs, rhs_h, preferred_element_type=F32)
                ybuf[r0:r0 + L, h0 * SSD_HEAD_DIM:h0 * SSD_HEAD_DIM + LANES] = y_pair
                w0 = jnp.broadcast_to(w_c[:, h0:h0 + 1], (L, LANES))
                w1 = jnp.broadcast_to(w_c[:, h0 + 1:h0 + 2], (L, LANES))
                xw_parts.append((x_pair * jnp.where(lo_half, w0, w1)).astype(BF16))
                c0v = jnp.broadcast_to(cd[:, h0:h0 + 1], (1, LANES))
                c1v = jnp.broadcast_to(cd[:, h0 + 1:h0 + 2], (1, LANES))
                cd_parts.append(jnp.where(lo_half, c0v, c1v))
            xw = jnp.concatenate(xw_parts, axis=1)
            cdx = jnp.concatenate(cd_parts, axis=1)
            bg_t = bg.T.astype(BF16)
            st_ref[g] = st_ref[g] * cdx + jnp.dot(bg_t, xw, preferred_element_type=F32)

    xs = jnp.concatenate([xcbuf[cb] for cb in range(D_SSD // LANES)], axis=1)
    y = (ybuf[...] + xs * dexp_ref[...]) * _silu(z_ref[...].astype(F32))
    gw = D_SSD // SSD_GROUPS
    for g in range(SSD_GROUPS):
        yg = y[:, g * gw:(g + 1) * gw]
        yn = yg * _rms_scale(yg) * nw_ref[:, g * gw:(g + 1) * gw]
        mix_ref[:, D_CONV + g * gw:D_CONV + (g + 1) * gw] = yn.astype(mix_ref.dtype)


def _mixer(p, dt_raw, cw, cb, lng, lnb, sw, sb, dtb, alog, dexp, nw, bsz, seq, ts=256):
    t = p.shape[0]
    nt = seq // ts
    row = lambda b, s: b * nt + s
    full = lambda shape: pl.BlockSpec(shape, lambda b, s: (0, 0))
    col_block = lambda j: pl.BlockSpec((ts, 1024), lambda b, s: (row(b, s), j))
    return pl.pallas_call(
        functools.partial(_mixer_kernel, ts=ts),
        name="mixer",
        out_shape=jax.ShapeDtypeStruct((t, D_MODEL), BF16),
        grid=(bsz, nt),
        in_specs=[
            col_block(0), col_block(1), col_block(2), col_block(3), col_block(4),
            pl.BlockSpec((ts, LANES), lambda b, s: (row(b, s), 0)),
            full((CONV_K, D_CONV)), full((1, D_CONV)), full((1, D_CONV)), full((1, D_CONV)),
            full((SSD_CONV_K, D_XBC)), full((1, D_XBC)),
            full((1, LANES)), full((1, LANES)), full((1, D_SSD)), full((1, D_SSD)),
        ],
        out_specs=pl.BlockSpec((ts, D_MODEL), lambda b, s: (row(b, s), 0)),
        scratch_shapes=[
            pltpu.VMEM((D_CONV // LANES, CONV_HALO + ts, LANES), F32),
            pltpu.VMEM((D_CONV // LANES, ts, LANES), F32),
            pltpu.VMEM((D_XBC // LANES, XBC_HALO + ts, LANES), F32),
            pltpu.VMEM((D_XBC // LANES, ts, LANES), F32),
            pltpu.VMEM((ts, D_SSD), F32),
            pltpu.VMEM((SSD_GROUPS, SSD_STATE, 4 * SSD_HEAD_DIM), F32),
        ],
        compiler_params=pltpu.CompilerParams(
            dimension_semantics=("arbitrary", "arbitrary"),
            vmem_limit_bytes=VMEM_LIMIT),
    )(p, p, p, p, p, dt_raw, cw, cb, lng, lnb, sw, sb, dtb, alog, dexp, nw)


def _outproj_kernel(mix_ref, w_ref, x_ref, gpost_ref, gpre_ref, x1_ref, h_ref):
    m = jnp.dot(mix_ref[...], w_ref[...], preferred_element_type=F32)
    x1 = x_ref[...] + m * _rms_scale(m) * gpost_ref[...]
    x1_ref[...] = x1
    h_ref[...] = (x1 * _rms_scale(x1) * gpre_ref[...]).astype(h_ref.dtype)


def _outproj(mix, w_out, x2, gpost, gpre, tm=512):
    t = mix.shape[0]
    return pl.pallas_call(
        _outproj_kernel,
        name="outproj",
        out_shape=(jax.ShapeDtypeStruct((t, D_MODEL), F32),
                   jax.ShapeDtypeStruct((t, D_MODEL), BF16)),
        grid=(t // tm,),
        in_specs=[
            pl.BlockSpec((tm, D_MODEL), lambda i: (i, 0)),
            pl.BlockSpec((D_MODEL, D_MODEL), lambda i: (0, 0)),
            pl.BlockSpec((tm, D_MODEL), lambda i: (i, 0)),
            pl.BlockSpec((1, D_MODEL), lambda i: (0, 0)),
            pl.BlockSpec((1, D_MODEL), lambda i: (0, 0)),
        ],
        out_specs=(pl.BlockSpec((tm, D_MODEL), lambda i: (i, 0)),
                   pl.BlockSpec((tm, D_MODEL), lambda i: (i, 0))),
        compiler_params=pltpu.CompilerParams(
            dimension_semantics=("arbitrary",),
            vmem_limit_bytes=VMEM_LIMIT),
    )(mix, w_out, x2, gpost, gpre)


def _ffn_kernel(h_ref, wg_ref, wu_ref, wd_ref, x1_hbm, gpost_ref, o_ref, x1_buf, x1_sem,
                *, tm, rows):
    i = pl.program_id(0)
    f = pl.program_id(1)
    x1_copy = pltpu.make_async_copy(
        x1_hbm.at[pl.ds(pl.multiple_of(i * tm, tm), tm), :], x1_buf, x1_sem)

    @pl.when(f == 0)
    def _():
        x1_copy.start()
        o_ref[...] = jnp.zeros(o_ref.shape, F32)

    h = h_ref[...]
    gate = jnp.dot(h, wg_ref[...], preferred_element_type=F32)
    up = jnp.dot(h, wu_ref[...], preferred_element_type=F32)
    act = (_silu(gate) * up).astype(BF16)
    o_ref[...] += jnp.dot(act, wd_ref[...], preferred_element_type=F32)

    @pl.when(f == pl.num_programs(1) - 1)
    def _():
        x1_copy.wait()

        def norm_rows(r, carry):
            sl = pl.ds(pl.multiple_of(r * rows, rows), rows)
            y = o_ref[sl, :]
            o_ref[sl, :] = x1_buf[sl, :] + y * _rms_scale(y) * gpost_ref[...]
            return carry

        lax.fori_loop(0, tm // rows, norm_rows, 0)


def _ffn(h, wg, wu, wd, x1, gpost, tm=1024, tf=512, rows=128):
    t = h.shape[0]
    return pl.pallas_call(
        functools.partial(_ffn_kernel, tm=tm, rows=rows),
        name="ffn",
        out_shape=jax.ShapeDtypeStruct((t, D_MODEL), F32),
        grid=(t // tm, D_FF // tf),
        in_specs=[
            pl.BlockSpec((tm, D_MODEL), lambda i, f: (i, 0)),
            pl.BlockSpec((D_MODEL, tf), lambda i, f: (0, f)),
            pl.BlockSpec((D_MODEL, tf), lambda i, f: (0, f)),
            pl.BlockSpec((tf, D_MODEL), lambda i, f: (f, 0)),
            pl.BlockSpec(memory_space=pl.ANY),
            pl.BlockSpec((1, D_MODEL), lambda i, f: (0, 0)),
        ],
        out_specs=pl.BlockSpec((tm, D_MODEL), lambda i, f: (i, 0)),
        scratch_shapes=[pltpu.VMEM((tm, D_MODEL), F32), pltpu.SemaphoreType.DMA(())],
        compiler_params=pltpu.CompilerParams(
            dimension_semantics=("arbitrary", "arbitrary"),
            vmem_limit_bytes=VMEM_LIMIT),
    )(h, wg, wu, wd, x1, gpost)


def kernel(x, norm_mix_pre, w_in, conv_dw_w, conv_dw_b, conv_ln_g, conv_ln_b,
           ssd_conv_w, ssd_conv_b, ssd_dt_bias, ssd_a_log, ssd_d, ssd_norm_w,
           w_out, norm_mix_post, norm_ffn_pre, w_gate, w_up, w_down, norm_ffn_post):
    bsz, seq, _ = x.shape
    depth = w_in.shape[0]
    x2 = x.reshape(bsz * seq, D_MODEL)
    row = lambda v: v.reshape(1, -1)
    pad = lambda v: jnp.pad(v, (0, LANES - v.shape[0])).reshape(1, LANES)
    for l in range(depth):
        w_main = w_in[l][:, :D_MAIN].astype(BF16)
        w_dt = jnp.pad(w_in[l][:, D_MAIN:], ((0, 0), (0, LANES - SSD_HEADS))).astype(BF16)
        p, dt_raw = _inproj(x2, row(norm_mix_pre[l]), w_main, w_dt)
        mix = _mixer(
            p, dt_raw, conv_dw_w[l], row(conv_dw_b[l]), row(conv_ln_g[l]), row(conv_ln_b[l]),
            ssd_conv_w[l], row(ssd_conv_b[l]), pad(ssd_dt_bias[l]), pad(ssd_a_log[l]),
            row(jnp.repeat(ssd_d[l], SSD_HEAD_DIM)), row(ssd_norm_w[l]), bsz, seq)
        x1, h = _outproj(mix, w_out[l].astype(BF16), x2,
                         row(norm_mix_post[l]), row(norm_ffn_pre[l]))
        x2 = _ffn(h, w_gate[l].astype(BF16), w_up[l].astype(BF16), w_down[l].astype(BF16),
                  x1, row(norm_ffn_post[l]))
    return x2.reshape(bsz, seq, D_MODEL)
```
